```python
import math
import jax, jax.numpy as jnp
from jax import lax
import numpy as np

D_MODEL = 2048
BATCH = 4
SEQ = 2048
DEPTH = 2
DEC_BATCH = 8
DEC_SEQ = 1
PAST_LEN = 16384
PAGE_SIZE = 128

HEAD_DIM = 128
H_DN = 8
H_MOBA = 8
W_DN = H_DN * HEAD_DIM
W_MOBA = H_MOBA * HEAD_DIM
W_MIX = W_DN + W_MOBA
CONV_W = 4
CONV_CH = 3 * W_DN
DN_CHUNK = 64
MOBA_BLOCK = 256
MOBA_TOPK = 3
MOBA_Q_CHUNK = 16
N_BUCKETS = 32
MAX_DISTANCE = 128
N_MEM = 256
H_X = 4
W_X = H_X * HEAD_DIM
D_FF = -(-(8 * D_MODEL) // (3 * 256)) * 256
ALPHA = (2 * DEPTH) ** 0.25
BETA_INIT = (8 * DEPTH) ** -0.25
LN_EPS = 1e-5
RMS_EPS = 1e-6
L2_EPS = 1e-6

OFF_ZA = 3 * W_DN
OFF_BA = 4 * W_DN
OFF_AA = OFF_BA + H_DN
OFF_QB = OFF_AA + H_DN
OFF_KB = OFF_QB + W_MOBA
OFF_VB = OFF_KB + W_MOBA
W_IN_COLS = OFF_VB + W_MOBA

kernel_name = 'hybrid_gdn_moba_decoder_step'


def layer_norm(x, g, b):
    xf = x.astype(jnp.float32)
    mu = xf.mean(-1, keepdims=True)
    var = jnp.square(xf - mu).mean(-1, keepdims=True)
    return ((xf - mu) * lax.rsqrt(var + LN_EPS) * g.astype(jnp.float32) + b.astype(jnp.float32)).astype(x.dtype)


def post_norm(x, sub, g, b):
    return layer_norm(ALPHA * x + sub, g, b)


def l2norm(x):
    return x * lax.rsqrt(jnp.sum(x * x, -1, keepdims=True) + L2_EPS)


def short_conv(x_pad, w):
    T = x_pad.shape[1] - (CONV_W - 1)
    y = sum(w[j] * x_pad[:, j:j + T] for j in range(CONV_W))
    return jax.nn.silu(y)


def dn_prep(conv_out, ba, aa, a_log, dt_bias):
    B, T, _ = conv_out.shape
    c = conv_out.astype(jnp.float32)
    q = l2norm(c[..., :W_DN].reshape(B, T, H_DN, HEAD_DIM)) * HEAD_DIM ** -0.5
    k = l2norm(c[..., W_DN:2 * W_DN].reshape(B, T, H_DN, HEAD_DIM))
    v = c[..., 2 * W_DN:].reshape(B, T, H_DN, HEAD_DIM)
    beta = jax.nn.sigmoid(ba.astype(jnp.float32))
    g = -jnp.exp(a_log.astype(jnp.float32)) * jax.nn.softplus(
        aa.astype(jnp.float32) + dt_bias.astype(jnp.float32))
    return q, k, v, beta, g


def gated_delta_chunked(q, k, v, beta, g):
    B, T, H, D = q.shape
    C = DN_CHUNK
    NC = T // C

    def blk(x):
        return jnp.moveaxis(x.reshape((B, NC, C) + x.shape[2:]), 3, 1)

    q, k, v, beta, g = blk(q), blk(k), blk(v), blk(beta), blk(g)
    gc = jnp.cumsum(g, axis=-1)
    pos = jnp.arange(C)
    tri_incl = pos[:, None] >= pos[None, :]
    tri_strict = pos[:, None] > pos[None, :]
    decay = jnp.exp(jnp.where(tri_incl, gc[..., :, None] - gc[..., None, :], -jnp.inf))
    kbeta = k * beta[..., None]
    L = jnp.where(tri_strict, jnp.einsum('bhncd,bhned->bhnce', kbeta, k) * decay, 0.0)
    rhs = jnp.concatenate([v * beta[..., None], kbeta * jnp.exp(gc)[..., None]], -1)
    sol = lax.linalg.triangular_solve(L, rhs, left_side=True, lower=True, unit_diagonal=True)
    u0, kcum = sol[..., :D], sol[..., D:]
    qk = jnp.einsum('bhncd,bhned->bhnce', q, k) * decay
    q_dec = q * jnp.exp(gc)[..., None]
    k_dec = k * jnp.exp(gc[..., -1:] - gc)[..., None]
    g_last = jnp.exp(gc[..., -1])

    def step(S, xs):
        u0_c, kcum_c, qk_c, qd_c, kd_c, gl_c = xs
        u = u0_c - jnp.einsum('bhcd,bhde->bhce', kcum_c, S)
        o = jnp.einsum('bhcd,bhde->bhce', qd_c, S) + jnp.einsum('bhce,bhed->bhcd', qk_c, u)
        S = S * gl_c[..., None, None] + jnp.einsum('bhcd,bhce->bhde', kd_c, u)
        return S, o

    xs = (jnp.moveaxis(u0, 2, 0), jnp.moveaxis(kcum, 2, 0), jnp.moveaxis(qk, 2, 0),
          jnp.moveaxis(q_dec, 2, 0), jnp.moveaxis(k_dec, 2, 0), jnp.moveaxis(g_last, 2, 0))
    S0 = jnp.zeros((B, H, D, D), jnp.float32)
    S, o = lax.scan(step, S0, xs)
    o = jnp.transpose(o, (1, 0, 3, 2, 4)).reshape(B, T, H, D)
    return o, S


def gated_delta_recurrent(S, q, k, v, beta, g):
    def step(S, xs):
        q_t, k_t, v_t, b_t, g_t = xs
        S = S * jnp.exp(g_t)[..., None, None]
        u = b_t[..., None] * (v_t - jnp.einsum('bhde,bhd->bhe', S, k_t))
        S = S + jnp.einsum('bhd,bhe->bhde', k_t, u)
        o = jnp.einsum('bhde,bhd->bhe', S, q_t)
        return S, o

    xs = (jnp.moveaxis(q, 1, 0), jnp.moveaxis(k, 1, 0), jnp.moveaxis(v, 1, 0),
          jnp.moveaxis(beta, 1, 0), jnp.moveaxis(g, 1, 0))
    S, o = lax.scan(step, S, xs)
    return jnp.moveaxis(o, 0, 1), S


def gated_rmsnorm(o, z, w):
    B, T = o.shape[:2]
    o = o * lax.rsqrt(jnp.mean(o * o, -1, keepdims=True) + RMS_EPS) * w.astype(jnp.float32)
    return o.reshape(B, T, W_DN) * jax.nn.silu(z.astype(jnp.float32))


def t5_bucket(dist):
    d = jnp.maximum(dist, 0)
    max_exact = N_BUCKETS // 2
    ratio = jnp.log(jnp.maximum(d, max_exact).astype(jnp.float32) / max_exact) / math.log(MAX_DISTANCE / max_exact)
    large = jnp.minimum(max_exact + (ratio * (N_BUCKETS - max_exact)).astype(jnp.int32), N_BUCKETS - 1)
    return jnp.where(d < max_exact, d, large)


def moba_qkv(proj):
    B, T = proj.shape[:2]
    q = proj[..., OFF_QB:OFF_KB].reshape(B, T, H_MOBA, HEAD_DIM)
    k = proj[..., OFF_KB:OFF_VB].reshape(B, T, H_MOBA, HEAD_DIM)
    v = proj[..., OFF_VB:W_IN_COLS].reshape(B, T, H_MOBA, HEAD_DIM)
    return q, k, v


def moba_blocks(k, v):
    B, L, H, D = k.shape
    NB = -(-L // MOBA_BLOCK)
    pad = ((0, 0), (0, NB * MOBA_BLOCK - L), (0, 0), (0, 0))
    kb = jnp.pad(k, pad).reshape(B, NB, MOBA_BLOCK, H, D).transpose(0, 3, 1, 2, 4)
    vb = jnp.pad(v, pad).reshape(B, NB, MOBA_BLOCK, H, D).transpose(0, 3, 1, 2, 4)
    kmean = kb.astype(jnp.float32).mean(3)
    return kb, vb, kmean


def moba_attend(q, kb, vb, kmean, q_pos, rel_bias):
    B, Q, H, D = q.shape
    NB = kb.shape[2]
    K = min(MOBA_TOPK, NB)
    qf = q.astype(jnp.float32)
    own = q_pos // MOBA_BLOCK
    gate = jnp.einsum('bqhd,bhnd->bqhn', qf, kmean)
    past = jnp.arange(NB)[None, :] < own[:, None]
    gate = jnp.where(past[None, :, None, :], gate, -jnp.inf)
    top_s, top_i = lax.top_k(gate, K)
    sel_ok = jnp.isfinite(top_s)
    idx = jnp.concatenate([top_i, jnp.broadcast_to(own[None, :, None, None], (B, Q, H, 1))], -1)
    bi = jnp.arange(B)[:, None, None, None]
    hi = jnp.arange(H)[None, None, :, None]
    kg = kb[bi, hi, idx].astype(jnp.float32)
    vg = vb[bi, hi, idx].astype(jnp.float32)
    s = jnp.einsum('bqhd,bqhtkd->bqhtk', qf, kg) * HEAD_DIM ** -0.5
    kpos = idx[..., None] * MOBA_BLOCK + jnp.arange(MOBA_BLOCK)
    dist = q_pos[None, :, None, None, None] - kpos
    ok = jnp.concatenate([jnp.broadcast_to(sel_ok[..., None], (B, Q, H, K, MOBA_BLOCK)),
                          dist[..., K:, :] >= 0], axis=3)
    s = s + rel_bias.astype(jnp.float32)[t5_bucket(dist), jnp.arange(H)[None, None, :, None, None]]
    s = jnp.where(ok, s, -jnp.inf)
    p = jax.nn.softmax(s.reshape(B, Q, H, -1), axis=-1).reshape(s.shape)
    return jnp.einsum('bqhtk,bqhtkd->bqhd', p, vg)


def moba_prompt(q, kb, vb, kmean, rel_bias):
    B, S, H, D = q.shape
    qc = q.reshape(B, S // MOBA_Q_CHUNK, MOBA_Q_CHUNK, H, D).transpose(1, 0, 2, 3, 4)
    pos = jnp.arange(S, dtype=jnp.int32).reshape(-1, MOBA_Q_CHUNK)
    o = lax.map(lambda a: moba_attend(a[0], kb, vb, kmean, a[1], rel_bias), (qc, pos))
    return o.transpose(1, 0, 2, 3, 4).reshape(B, S, H, D)


def merge_groups(oa, ob, w_out_l, dtype):
    B, T = oa.shape[:2]
    o = jnp.concatenate([oa, ob.reshape(B, T, W_MOBA)], -1).astype(dtype)
    return o @ w_out_l


def prompt_mixer(x, w_in_l, conv_w, a_log, dt_bias, norm_w, w_out_l, rel_bias):
    proj = x @ w_in_l
    x_pad = jnp.pad(proj[..., :CONV_CH], ((0, 0), (CONV_W - 1, 0), (0, 0)))
    conv_state = x_pad[:, -(CONV_W - 1):]
    q, k, v, beta, g = dn_prep(short_conv(x_pad, conv_w), proj[..., OFF_BA:OFF_AA],
                               proj[..., OFF_AA:OFF_QB], a_log, dt_bias)
    oa, S_fin = gated_delta_chunked(q, k, v, beta, g)
    oa = gated_rmsnorm(oa, proj[..., OFF_ZA:OFF_BA], norm_w)
    qb, kb, vb = moba_qkv(proj)
    kbl, vbl, kmean = moba_blocks(kb, vb)
    ob = moba_prompt(qb, kbl, vbl, kmean, rel_bias)
    return merge_groups(oa, ob, w_out_l, x.dtype), kb, vb, S_fin, conv_state


def sample_mixer(x, k_pool, v_pool, page_table, S0, conv_state, w_in_l, conv_w, a_log, dt_bias,
                 norm_w, w_out_l, rel_bias):
    B, T, _ = x.shape
    proj = x @ w_in_l
    x_pad = jnp.concatenate([conv_state.astype(proj.dtype), proj[..., :CONV_CH]], 1)
    new_conv = x_pad[:, -(CONV_W - 1):]
    q, k, v, beta, g = dn_prep(short_conv(x_pad, conv_w), proj[..., OFF_BA:OFF_AA],
                               proj[..., OFF_AA:OFF_QB], a_log, dt_bias)
    oa, S_new = gated_delta_recurrent(S0.astype(jnp.float32), q, k, v, beta, g)
    oa = gated_rmsnorm(oa, proj[..., OFF_ZA:OFF_BA], norm_w)
    qb, kb, vb = moba_qkv(proj)
    past_k = k_pool[page_table].reshape(B, -1, H_MOBA, HEAD_DIM)
    past_v = v_pool[page_table].reshape(B, -1, H_MOBA, HEAD_DIM)
    past_len = past_k.shape[1]
    k_full = jnp.concatenate([past_k, kb.astype(past_k.dtype)], 1)
    v_full = jnp.concatenate([past_v, vb.astype(past_v.dtype)], 1)
    kbl, vbl, kmean = moba_blocks(k_full, v_full)
    q_pos = past_len + jnp.arange(T, dtype=jnp.int32)
    ob = moba_attend(qb, kbl, vbl, kmean, q_pos, rel_bias)
    return merge_groups(oa, ob, w_out_l, x.dtype), kb, vb, S_new, new_conv


def mem_kv(mem, w_kv):
    B = mem.shape[0]
    kv = mem @ w_kv
    return (kv[..., :W_X].reshape(B, N_MEM, H_X, HEAD_DIM),
            kv[..., W_X:].reshape(B, N_MEM, H_X, HEAD_DIM))


def cross_attend(h, mk, mv, w_q, w_o):
    B, T, _ = h.shape
    q = (h @ w_q).reshape(B, T, H_X, HEAD_DIM).astype(jnp.float32)
    s = jnp.einsum('bthd,bnhd->bhtn', q, mk.astype(jnp.float32)) * HEAD_DIM ** -0.5
    p = jax.nn.softmax(s, axis=-1)
    o = jnp.einsum('bhtn,bnhd->bthd', p, mv.astype(jnp.float32)).reshape(B, T, W_X).astype(h.dtype)
    return o @ w_o


def swiglu(h, w1, w2):
    gu = h @ w1
    return (jax.nn.silu(gu[..., :D_FF]) * gu[..., D_FF:]) @ w2


def setup_inputs(seed: int = 0) -> dict:
    key = jax.random.key(seed)
    ks = jax.random.split(key, 24)
    f32 = jnp.float32
    n_pages = PAST_LEN // PAGE_SIZE
    n_used = DEC_BATCH * n_pages
    n_pool = n_used + max(1, n_used // 4)

    def nrm(k, shape, scale=1.0):
        return scale * jax.random.normal(k, shape, f32)

    page_table = jax.random.permutation(ks[6], n_pool)[:n_used].reshape(DEC_BATCH, n_pages).astype(jnp.int32)
    dt = jnp.exp(jax.random.uniform(ks[10], (DEPTH, H_DN), f32, math.log(1e-3), math.log(1e-1)))
    return {
        'x_prompt': nrm(ks[0], (BATCH, SEQ, D_MODEL)),
        'x_sample': nrm(ks[1], (DEC_BATCH, DEC_SEQ, D_MODEL)),
        'cache_k': nrm(ks[2], (DEPTH, n_pool, PAGE_SIZE, H_MOBA, HEAD_DIM)),
        'cache_v': nrm(ks[3], (DEPTH, n_pool, PAGE_SIZE, H_MOBA, HEAD_DIM)),
        'cache_mem_k': nrm(ks[4], (DEPTH, DEC_BATCH, N_MEM, H_X, HEAD_DIM)),
        'cache_mem_v': nrm(ks[5], (DEPTH, DEC_BATCH, N_MEM, H_X, HEAD_DIM)),
        'state_dn': nrm(ks[7], (DEPTH, DEC_BATCH, H_DN, HEAD_DIM, HEAD_DIM), 0.1),
        'state_conv': nrm(ks[8], (DEPTH, DEC_BATCH, CONV_W - 1, CONV_CH)),
        'page_table': page_table,
        'mem_prompt': nrm(ks[9], (BATCH, N_MEM, D_MODEL)),
        'w_in': nrm(ks[11], (DEPTH, D_MODEL, W_IN_COLS), D_MODEL ** -0.5),
        'dn_conv_w': nrm(ks[12], (DEPTH, CONV_W, CONV_CH), CONV_W ** -0.5),
        'dn_a_log': jnp.log(jax.random.uniform(ks[13], (DEPTH, H_DN), f32, 1.0, 16.0)),
        'dn_dt_bias': dt + jnp.log(-jnp.expm1(-dt)),
        'dn_norm_w': 1.0 + nrm(ks[14], (DEPTH, HEAD_DIM), 0.1),
        'w_out': nrm(ks[15], (DEPTH, W_MIX, D_MODEL), W_MIX ** -0.5 * BETA_INIT),
        'rel_bias': nrm(ks[16], (N_BUCKETS, H_MOBA), 0.2),
        'x_wq': nrm(ks[17], (DEPTH, D_MODEL, W_X), D_MODEL ** -0.5),
        'x_wkv': nrm(ks[18], (DEPTH, D_MODEL, 2 * W_X), D_MODEL ** -0.5),
        'x_wo': nrm(ks[19], (DEPTH, W_X, D_MODEL), W_X ** -0.5 * BETA_INIT),
        'ffn_w1': nrm(ks[20], (DEPTH, D_MODEL, 2 * D_FF), D_MODEL ** -0.5),
        'ffn_w2': nrm(ks[21], (DEPTH, D_FF, D_MODEL), D_FF ** -0.5 * BETA_INIT),
        'ln_g': 1.0 + nrm(ks[22], (DEPTH, 3, D_MODEL), 0.05),
        'ln_b': nrm(ks[23], (DEPTH, 3, D_MODEL), 0.02),
    }


def reference(x_prompt, x_sample, cache_k, cache_v, cache_mem_k, cache_mem_v, state_dn, state_conv,
              page_table, mem_prompt, w_in, dn_conv_w, dn_a_log, dn_dt_bias, dn_norm_w, w_out, rel_bias,
              x_wq, x_wkv, x_wo, ffn_w1, ffn_w2, ln_g, ln_b):
    xp, xs = x_prompt, x_sample
    kp_l, vp_l, ks_l, vs_l, mkp_l, mvp_l, sp_l, ss_l, cp_l, cs_l = [], [], [], [], [], [], [], [], [], []
    for l in range(DEPTH):
        mix_p, kp, vp, sp, cp = prompt_mixer(xp, w_in[l], dn_conv_w[l], dn_a_log[l], dn_dt_bias[l],
                                             dn_norm_w[l], w_out[l], rel_bias)
        mix_s, k_s, v_s, s_s, c_s = sample_mixer(xs, cache_k[l], cache_v[l], page_table, state_dn[l],
                                                 state_conv[l], w_in[l], dn_conv_w[l], dn_a_log[l],
                                                 dn_dt_bias[l], dn_norm_w[l], w_out[l], rel_bias)
        xp = post_norm(xp, mix_p, ln_g[l, 0], ln_b[l, 0])
        xs = post_norm(xs, mix_s, ln_g[l, 0], ln_b[l, 0])
        mkp, mvp = mem_kv(mem_prompt, x_wkv[l])
        xp = post_norm(xp, cross_attend(xp, mkp, mvp, x_wq[l], x_wo[l]), ln_g[l, 1], ln_b[l, 1])
        xs = post_norm(xs, cross_attend(xs, cache_mem_k[l], cache_mem_v[l], x_wq[l], x_wo[l]),
                       ln_g[l, 1], ln_b[l, 1])
        xp = post_norm(xp, swiglu(xp, ffn_w1[l], ffn_w2[l]), ln_g[l, 2], ln_b[l, 2])
        xs = post_norm(xs, swiglu(xs, ffn_w1[l], ffn_w2[l]), ln_g[l, 2], ln_b[l, 2])
        kp_l.append(kp); vp_l.append(vp); ks_l.append(k_s); vs_l.append(v_s)
        mkp_l.append(mkp); mvp_l.append(mvp)
        sp_l.append(sp.astype(x_prompt.dtype)); ss_l.append(s_s.astype(state_dn.dtype))
        cp_l.append(cp); cs_l.append(c_s)
    return (xp, xs, jnp.stack(kp_l), jnp.stack(vp_l), jnp.stack(ks_l), jnp.stack(vs_l),
            jnp.stack(mkp_l), jnp.stack(mvp_l), jnp.stack(sp_l), jnp.stack(ss_l),
            jnp.stack(cp_l), jnp.stack(cs_l))
```

```python
import functools
import math

import jax
import jax.numpy as jnp
from jax import lax
from jax.experimental import pallas as pl
from jax.experimental.pallas import tpu as pltpu

F32 = jnp.float32
BF16 = jnp.bfloat16

D_MODEL = 2048
DEPTH = 2
HEAD_DIM = 128
H_DN = 8
H_MOBA = 8
W_DN = H_DN * HEAD_DIM
W_MOBA = H_MOBA * HEAD_DIM
CONV_W = 4
CONV_CH = 3 * W_DN
DN_CHUNK = 64
MOBA_BLOCK = 256
MOBA_TOPK = 3
N_BUCKETS = 32
MAX_DISTANCE = 128
N_MEM = 256
H_X = 4
W_X = H_X * HEAD_DIM
D_FF = 5632
PAGE_SIZE = 128
ALPHA = (2 * DEPTH) ** 0.25
LN_EPS = 1e-5
RMS_EPS = 1e-6
L2_EPS = 1e-6
OFF_BA = 4 * W_DN
OFF_QB = OFF_BA + 2 * H_DN
W_MAIN = 4 * W_DN + 3 * W_MOBA

LANES = 128
SUBLANES = 8
VMEM_LIMIT = 56 * 1024 * 1024

SAMPLE_ROWS = SUBLANES

_HI = lax.Precision.HIGHEST


def _cparams(sem):
    return pltpu.CompilerParams(dimension_semantics=sem, vmem_limit_bytes=VMEM_LIMIT)


def _dot(a, b, precision=None):
    return jnp.dot(a, b, preferred_element_type=F32, precision=precision)


def _dot_nt(a, b, precision=None):
    return lax.dot_general(a, b, (((1,), (1,)), ((), ())), preferred_element_type=F32, precision=precision)


def _dot_tn(a, b, precision=None):
    return lax.dot_general(a, b, (((0,), (0,)), ((), ())), preferred_element_type=F32, precision=precision)


def _sigmoid(x):
    return 1.0 / (1.0 + jnp.exp(-x))


def _silu(x):
    return x * _sigmoid(x)


def _softplus(x):
    return jnp.maximum(x, 0.0) + jnp.log1p(jnp.exp(-jnp.abs(x)))


def _layer_norm(y, g, b):
    mu = jnp.mean(y, axis=-1, keepdims=True)
    yc = y - mu
    var = jnp.mean(yc * yc, axis=-1, keepdims=True)
    return yc * lax.rsqrt(var + LN_EPS) * g + b


def _mm_kernel(x_ref, w_ref, o_ref):
    o_ref[...] = _dot(x_ref[...].astype(BF16), w_ref[...]).astype(o_ref.dtype)


def _matmul(x, w, *, tm, tn, out_dtype):
    m, k = x.shape
    n = w.shape[1]
    assert m % tm == 0 and n % tn == 0
    return pl.pallas_call(
        _mm_kernel,
        grid=(n // tn, m // tm),
        in_specs=[pl.BlockSpec((tm, k), lambda j, i: (i, 0)),
                  pl.BlockSpec((k, tn), lambda j, i: (0, j))],
        out_specs=pl.BlockSpec((tm, tn), lambda j, i: (i, j)),
        out_shape=jax.ShapeDtypeStruct((m, n), out_dtype),
        compiler_params=_cparams(("arbitrary", "arbitrary")),
        name="matmul",
    )(x, w)


def _mm2_ln_kernel(a_ref, b_ref, wa_ref, wb_ref, r_ref, g_ref, beta_ref, of_ref, ob_ref):
    acc = _dot(a_ref[...], wa_ref[...]) + _dot(b_ref[...], wb_ref[...])
    y = _layer_norm(ALPHA * r_ref[...] + acc, g_ref[...], beta_ref[...])
    of_ref[...] = y
    ob_ref[...] = y.astype(BF16)


def _mm2_ln(a, b, wa, wb, resid, g, beta, *, tm):
    m, ka = a.shape
    kb = b.shape[1]
    n = wa.shape[1]
    row = lambda i: (i, 0)
    fixed = lambda i: (0, 0)
    return pl.pallas_call(
        _mm2_ln_kernel,
        grid=(m // tm,),
        in_specs=[pl.BlockSpec((tm, ka), row), pl.BlockSpec((tm, kb), row),
                  pl.BlockSpec((ka, n), fixed), pl.BlockSpec((kb, n), fixed),
                  pl.BlockSpec((tm, n), row), pl.BlockSpec((1, n), fixed), pl.BlockSpec((1, n), fixed)],
        out_specs=[pl.BlockSpec((tm, n), row), pl.BlockSpec((tm, n), row)],
        out_shape=[jax.ShapeDtypeStruct((m, n), F32), jax.ShapeDtypeStruct((m, n), BF16)],
        compiler_params=_cparams(("arbitrary",)),
        name="mix_out_ln",
    )(a, b, wa, wb, resid, g, beta)


def _cross_kernel(xb_ref, x_ref, wq_ref, mk_ref, mv_ref, wo_ref, g_ref, beta_ref, of_ref, ob_ref):
    q = _dot(xb_ref[...], wq_ref[...]).astype(BF16)
    mk = mk_ref[...].astype(BF16)
    mv = mv_ref[...].astype(BF16)
    heads = []
    for h in range(H_X):
        sl = slice(h * HEAD_DIM, (h + 1) * HEAD_DIM)
        s = _dot_nt(q[:, sl], mk[:, sl]) * HEAD_DIM ** -0.5
        s = s - jnp.max(s, axis=-1, keepdims=True)
        e = jnp.exp(s)
        p = e / jnp.sum(e, axis=-1, keepdims=True)
        heads.append(_dot(p.astype(BF16), mv[:, sl]))
    o = jnp.concatenate(heads, axis=-1).astype(BF16)
    y = _layer_norm(ALPHA * x_ref[...] + _dot(o, wo_ref[...]), g_ref[...], beta_ref[...])
    of_ref[...] = y
    ob_ref[...] = y.astype(BF16)


def _cross_attention(xb, x, wq, mk, mv, wo, g, beta, *, n_batch, tq):
    m = x.shape[0]
    per_batch = m // n_batch // tq
    row = lambda i: (i, 0)
    fixed = lambda i: (0, 0)
    mk_arr, mk_lead = mk
    mv_arr, mv_lead = mv
    mem_spec = lambda lead: pl.BlockSpec((None,) * len(lead) + (None, N_MEM, W_X),
                                         lambda i: lead + (i // per_batch, 0, 0))
    return pl.pallas_call(
        _cross_kernel,
        grid=(m // tq,),
        in_specs=[pl.BlockSpec((tq, D_MODEL), row), pl.BlockSpec((tq, D_MODEL), row),
                  pl.BlockSpec((D_MODEL, W_X), fixed), mem_spec(mk_lead), mem_spec(mv_lead),
                  pl.BlockSpec((W_X, D_MODEL), fixed),
                  pl.BlockSpec((1, D_MODEL), fixed), pl.BlockSpec((1, D_MODEL), fixed)],
        out_specs=[pl.BlockSpec((tq, D_MODEL), row), pl.BlockSpec((tq, D_MODEL), row)],
        out_shape=[jax.ShapeDtypeStruct((m, D_MODEL), F32), jax.ShapeDtypeStruct((m, D_MODEL), BF16)],
        compiler_params=_cparams(("arbitrary",)),
        name="cross_attention",
    )(xb, x, wq, mk_arr, mv_arr, wo, g, beta)


def _ffn_kernel(xb_ref, x_ref, wg_ref, wu_ref, w2_ref, g_ref, beta_ref, of_ref, ob_ref, acc_ref):
    f = pl.program_id(1)
    xb = xb_ref[...]
    hid = _silu(_dot(xb, wg_ref[...])) * _dot(xb, wu_ref[...])
    part = _dot(hid.astype(BF16), w2_ref[...])

    @pl.when(f == 0)
    def _():
        acc_ref[...] = part

    @pl.when(f > 0)
    def _():
        acc_ref[...] += part

    @pl.when(f == pl.num_programs(1) - 1)
    def _():
        y = _layer_norm(ALPHA * x_ref[...] + acc_ref[...], g_ref[...], beta_ref[...])
        of_ref[...] = y
        ob_ref[...] = y.astype(BF16)


def _ffn(xb, x, w1, w2, g, beta, *, tm, fc):
    m = x.shape[0]
    nf = D_FF // fc
    row = lambda i, f: (i, 0)
    fixed = lambda i, f: (0, 0)
    return pl.pallas_call(
        _ffn_kernel,
        grid=(m // tm, nf),
        in_specs=[pl.BlockSpec((tm, D_MODEL), row), pl.BlockSpec((tm, D_MODEL), row),
                  pl.BlockSpec((D_MODEL, fc), lambda i, f: (0, f)),
                  pl.BlockSpec((D_MODEL, fc), lambda i, f: (0, nf + f)),
                  pl.BlockSpec((fc, D_MODEL), lambda i, f: (f, 0)),
                  pl.BlockSpec((1, D_MODEL), fixed), pl.BlockSpec((1, D_MODEL), fixed)],
        out_specs=[pl.BlockSpec((tm, D_MODEL), row), pl.BlockSpec((tm, D_MODEL), row)],
        out_shape=[jax.ShapeDtypeStruct((m, D_MODEL), F32), jax.ShapeDtypeStruct((m, D_MODEL), BF16)],
        scratch_shapes=[pltpu.VMEM((tm, D_MODEL), F32)],
        compiler_params=_cparams(("arbitrary", "arbitrary")),
        name="ffn",
    )(xb, x, w1, w1, w2, g, beta)


def _t5_bucket(d):
    d = jnp.maximum(d, 0)
    max_exact = N_BUCKETS // 2
    ratio = jnp.log(jnp.maximum(d, max_exact).astype(F32) / max_exact) / math.log(MAX_DISTANCE / max_exact)
    large = jnp.minimum(max_exact + (ratio * (N_BUCKETS - max_exact)).astype(jnp.int32), N_BUCKETS - 1)
    return jnp.where(d < max_exact, d, large)


def _bias_kernel(rb_ref, o_ref):
    h = pl.program_id(0)
    i = lax.broadcasted_iota(jnp.int32, (MOBA_BLOCK, MOBA_BLOCK), 0)
    j = lax.broadcasted_iota(jnp.int32, (MOBA_BLOCK, MOBA_BLOCK), 1)
    for which in range(2):
        bucket = _t5_bucket(which * MOBA_BLOCK + i - j)
        tile = jnp.zeros((MOBA_BLOCK, MOBA_BLOCK), F32)
        for b in range(N_BUCKETS):
            tile = jnp.where(bucket == b, rb_ref[b, h], tile)
        o_ref[which] = tile


def _bias_tiles(rel_bias):
    return pl.pallas_call(
        _bias_kernel,
        grid=(H_MOBA,),
        in_specs=[pl.BlockSpec(memory_space=pltpu.SMEM)],
        out_specs=pl.BlockSpec((None, 2, MOBA_BLOCK, MOBA_BLOCK), lambda h: (h, 0, 0, 0)),
        out_shape=jax.ShapeDtypeStruct((H_MOBA, 2, MOBA_BLOCK, MOBA_BLOCK), F32),
        compiler_params=_cparams(("arbitrary",)),
        name="rel_bias_tiles",
    )(rel_bias)


def _head_column(x, col):
    lane = lax.broadcasted_iota(jnp.int32, x.shape, 1)
    return jnp.sum(jnp.where(lane == col, x, 0.0), axis=-1, keepdims=True)


def _conv_silu(xpad_ref, w, rows):
    y = w[3:4] * xpad_ref[SUBLANES:SUBLANES + rows, :]
    for tap in range(CONV_W - 1):
        off = SUBLANES - (CONV_W - 1) + tap
        y = y + w[tap:tap + 1] * xpad_ref[off:off + rows, :]
    return _silu(y)


def _l2norm(x):
    return x * lax.rsqrt(jnp.sum(x * x, axis=-1, keepdims=True) + L2_EPS)


def _gated_rmsnorm(o, z, w):
    return o * lax.rsqrt(jnp.mean(o * o, axis=-1, keepdims=True) + RMS_EPS) * w * _silu(z)


def _gdn_chunk(q, k, v, beta, g, s):
    c = DN_CHUNK
    ri = lax.broadcasted_iota(jnp.int32, (c, c), 0)
    ci = lax.broadcasted_iota(jnp.int32, (c, c), 1)
    incl = ri >= ci
    incl_f = incl.astype(F32)
    g_lanes = jnp.broadcast_to(g, (c, HEAD_DIM))
    gc = _dot(incl_f, g_lanes, _HI)
    gc_row = _dot_tn(g_lanes[:, :c], (ri <= ci).astype(F32), _HI)
    decay = jnp.where(incl, jnp.exp(gc[:, :c] - gc_row), 0.0)
    kb = k * beta
    k16 = k.astype(BF16)
    lmat = jnp.where(ri > ci, _dot_nt(kb.astype(BF16), k16) * decay, 0.0)
    eye = (ri == ci).astype(F32)
    inv = eye - lmat
    pw = lmat
    for _ in range(5):
        pw = _dot(pw, pw, _HI)
        inv = inv + _dot(inv, pw, _HI)
    eg = jnp.exp(gc)
    rhs = jnp.concatenate([v * beta, kb * eg], axis=-1)
    sol = _dot(inv, rhs, _HI)
    u0, kcum = sol[:, :HEAD_DIM], sol[:, HEAD_DIM:]
    qk = _dot_nt(q.astype(BF16), k16) * decay
    g_last = gc[c - 1:c, :]
    k_dec = k * jnp.exp(g_last - gc)
    s16 = s.astype(BF16)
    u = u0 - _dot(kcum.astype(BF16), s16)
    u16 = u.astype(BF16)
    o = _dot((q * eg).astype(BF16), s16) + _dot(qk.astype(BF16), u16)
    s_new = s * jnp.exp(g_last) + _dot_tn(k_dec.astype(BF16), u16)
    return o, s_new


def _gdn_kernel(alog_ref, dtb_ref, pq_ref, pk_ref, pv_ref, pz_ref, ba_ref, wq_ref, wk_ref, wv_ref, nw_ref,
                o_ref, sfin_ref, s_ref, xq_ref, xk_ref, xv_ref, *, rows):
    h = pl.program_id(1)
    t = pl.program_id(2)
    pads = ((xq_ref, pq_ref), (xk_ref, pk_ref), (xv_ref, pv_ref))

    @pl.when(t == 0)
    def _():
        s_ref[...] = jnp.zeros_like(s_ref)
        for xp, _ in pads:
            xp[0:SUBLANES, :] = jnp.zeros((SUBLANES, HEAD_DIM), F32)

    @pl.when(t > 0)
    def _():
        for xp, _ in pads:
            xp[0:SUBLANES, :] = xp[rows:rows + SUBLANES, :]

    for xp, src in pads:
        xp[SUBLANES:SUBLANES + rows, :] = src[...]

    q = _l2norm(_conv_silu(xq_ref, wq_ref[...], rows)) * HEAD_DIM ** -0.5
    k = _l2norm(_conv_silu(xk_ref, wk_ref[...], rows))
    v = _conv_silu(xv_ref, wv_ref[...], rows)
    logits = ba_ref[...]
    beta = _sigmoid(_head_column(logits, h))
    a_neg = -jnp.exp(jnp.full((1, 1), alog_ref[h], F32))
    g = a_neg * _softplus(_head_column(logits, h + H_DN) + dtb_ref[h])
    z = pz_ref[...]
    nw = nw_ref[...]
    s = s_ref[...]
    for c in range(rows // DN_CHUNK):
        sl = slice(c * DN_CHUNK, (c + 1) * DN_CHUNK)
        o, s = _gdn_chunk(q[sl], k[sl], v[sl], beta[sl], g[sl], s)
        o_ref[sl, :] = _gated_rmsnorm(o, z[sl], nw).astype(BF16)
    s_ref[...] = s

    @pl.when(t == pl.num_programs(2) - 1)
    def _():
        sfin_ref[...] = s


def _gdn_prompt(proj, logits, conv_w, a_log, dt_bias, norm_w, *, n_batch, seq, rows):
    nt = seq // rows
    m = n_batch * seq
    col = lambda off: pl.BlockSpec((rows, HEAD_DIM), lambda b, h, t: (b * nt + t, h + off))
    wcol = lambda off: pl.BlockSpec((CONV_W, HEAD_DIM), lambda b, h, t: (0, h + off))
    smem = pl.BlockSpec(memory_space=pltpu.SMEM)
    return pl.pallas_call(
        functools.partial(_gdn_kernel, rows=rows),
        grid=(n_batch, H_DN, nt),
        in_specs=[smem, smem, col(0), col(H_DN), col(2 * H_DN), col(3 * H_DN),
                  pl.BlockSpec((rows, LANES), lambda b, h, t: (b * nt + t, 0)),
                  wcol(0), wcol(H_DN), wcol(2 * H_DN),
                  pl.BlockSpec((1, HEAD_DIM), lambda b, h, t: (0, 0))],
        out_specs=[pl.BlockSpec((rows, HEAD_DIM), lambda b, h, t: (b * nt + t, h)),
                   pl.BlockSpec((None, None, HEAD_DIM, HEAD_DIM), lambda b, h, t: (b, h, 0, 0))],
        out_shape=[jax.ShapeDtypeStruct((m, W_DN), BF16),
                   jax.ShapeDtypeStruct((n_batch, H_DN, HEAD_DIM, HEAD_DIM), F32)],
        scratch_shapes=[pltpu.VMEM((HEAD_DIM, HEAD_DIM), F32)]
        + [pltpu.VMEM((rows + SUBLANES, HEAD_DIM), F32)] * 3,
        compiler_params=_cparams(("arbitrary", "arbitrary", "arbitrary")),
        name="gdn_prompt",
    )(a_log, dt_bias, proj, proj, proj, proj, logits, conv_w, conv_w, conv_w, norm_w)


def _moba_kernel(far_ref, q_ref, k_ref, v_ref, bias_ref, o_ref, k16_ref, v16_ref, kmean_ref, *, n_blocks):
    h = pl.program_id(1)
    n = pl.program_id(2)
    blk = MOBA_BLOCK

    @pl.when(n == 0)
    def _():
        kmean_ref[...] = jnp.zeros_like(kmean_ref)
        for j in range(n_blocks):
            kj = k_ref[j * blk:(j + 1) * blk, :]
            kmean_ref[j:j + 1, :] = jnp.mean(kj, axis=0, keepdims=True)
            k16_ref[j * blk:(j + 1) * blk, :] = kj.astype(BF16)
            v16_ref[j * blk:(j + 1) * blk, :] = v_ref[j * blk:(j + 1) * blk, :].astype(BF16)

    q = q_ref[...]
    q16 = q.astype(BF16)
    gate = _dot_nt(q, kmean_ref[...], _HI)
    lane = lax.broadcasted_iota(jnp.int32, gate.shape, 1)
    gate = jnp.where(lane < n, gate, -jnp.inf)
    rank = jnp.zeros(gate.shape, jnp.int32)
    for j in range(n_blocks - 1):
        other = gate[:, j:j + 1]
        ahead = (other > gate) | ((other == gate) & (j < lane))
        rank = rank + jnp.where(ahead, 1, 0)
    chosen = jnp.where((rank < MOBA_TOPK) & (lane < n), 1.0, 0.0)

    scale = HEAD_DIM ** -0.5
    row = lax.broadcasted_iota(jnp.int32, (blk, blk), 0)
    colm = lax.broadcasted_iota(jnp.int32, (blk, blk), 1)
    own = pl.multiple_of(n * blk, blk)
    s = _dot_nt(q16, k16_ref[pl.ds(own, blk), :]) * scale + bias_ref[0]
    s = jnp.where(row >= colm, s, -jnp.inf)
    m0 = jnp.max(s, axis=-1, keepdims=True)
    p = jnp.exp(s - m0)
    l0 = jnp.sum(p, axis=-1, keepdims=True)
    acc0 = _dot(p.astype(BF16), v16_ref[pl.ds(own, blk), :])
    far = far_ref[h]

    def past_block(j, carry):
        m_i, l_i, acc = carry
        start = pl.multiple_of(j * blk, blk)
        bias = jnp.where(j == n - 1, bias_ref[1], far)
        sj = _dot_nt(q16, k16_ref[pl.ds(start, blk), :]) * scale + bias
        picked = jnp.sum(jnp.where(lane == j, chosen, 0.0), axis=-1, keepdims=True) > 0.5
        sj = jnp.where(picked, sj, -jnp.inf)
        m_new = jnp.maximum(m_i, jnp.max(sj, axis=-1, keepdims=True))
        corr = jnp.exp(m_i - m_new)
        pj = jnp.exp(sj - m_new)
        l_new = corr * l_i + jnp.sum(pj, axis=-1, keepdims=True)
        acc_new = corr * acc + _dot(pj.astype(BF16), v16_ref[pl.ds(start, blk), :])
        return m_new, l_new, acc_new

    _, l_f, acc_f = lax.fori_loop(0, n, past_block, (m0, l0, acc0))
    o_ref[...] = (acc_f / l_f).astype(BF16)


def _moba_prompt(proj, bias_tiles, far_bias, *, n_batch, seq):
    nb = seq // MOBA_BLOCK
    m = n_batch * seq
    first_q = (4 * W_DN) // HEAD_DIM
    whole = lambda off: pl.BlockSpec((seq, HEAD_DIM), lambda b, h, n: (b, first_q + off + h))
    return pl.pallas_call(
        functools.partial(_moba_kernel, n_blocks=nb),
        grid=(n_batch, H_MOBA, nb),
        in_specs=[pl.BlockSpec(memory_space=pltpu.SMEM),
                  pl.BlockSpec((MOBA_BLOCK, HEAD_DIM), lambda b, h, n: (b * nb + n, first_q + h)),
                  whole(H_MOBA), whole(2 * H_MOBA),
                  pl.BlockSpec((None, 2, MOBA_BLOCK, MOBA_BLOCK), lambda b, h, n: (h, 0, 0, 0))],
        out_specs=pl.BlockSpec((MOBA_BLOCK, HEAD_DIM), lambda b, h, n: (b * nb + n, h)),
        out_shape=jax.ShapeDtypeStruct((m, W_MOBA), BF16),
        scratch_shapes=[pltpu.VMEM((seq, HEAD_DIM), BF16), pltpu.VMEM((seq, HEAD_DIM), BF16),
                        pltpu.VMEM((LANES, HEAD_DIM), F32)],
        compiler_params=_cparams(("arbitrary", "arbitrary", "arbitrary")),
        name="moba_prompt",
    )(far_bias, proj, proj, proj, bias_tiles)


def _gdn_step_kernel(alog_ref, dtb_ref, x_ref, z_ref, ba_ref, cs_ref, w_ref, nw_ref, s0_ref,
                     o_ref, s_ref, nc_ref):
    w = w_ref[...]
    cs = cs_ref[...]
    x_new = x_ref[0:1, :]
    y = w[CONV_W - 1:CONV_W] * x_new
    for tap in range(CONV_W - 1):
        y = y + w[tap:tap + 1] * cs[tap:tap + 1]
    y = _silu(y)
    nc_ref[0:CONV_W - 2, :] = cs[1:CONV_W - 1]
    nc_ref[CONV_W - 2:CONV_W - 1, :] = x_new
    logits = ba_ref[0:1, :]
    z = z_ref[0:1, :]
    nw = nw_ref[...]
    first_row = lax.broadcasted_iota(jnp.int32, (SUBLANES, HEAD_DIM), 0) == 0
    outs = []
    for h in range(H_DN):
        sl = lambda base: slice(base + h * HEAD_DIM, base + (h + 1) * HEAD_DIM)
        q = _l2norm(y[:, sl(0)]) * HEAD_DIM ** -0.5
        k = _l2norm(y[:, sl(W_DN)])
        v = y[:, sl(2 * W_DN)]
        beta = _sigmoid(logits[:, h:h + 1])
        a_neg = -jnp.exp(jnp.full((1, 1), alog_ref[h], F32))
        g = a_neg * _softplus(logits[:, H_DN + h:H_DN + h + 1] + dtb_ref[h])
        s = s0_ref[h] * jnp.exp(g)
        k8 = jnp.broadcast_to(k, (SUBLANES, HEAD_DIM))
        u = beta * (v - _dot(k8, s, _HI)[0:1])
        u8 = jnp.broadcast_to(u, (SUBLANES, HEAD_DIM))
        s = s + _dot_tn(jnp.where(first_row, k8, 0.0), u8, _HI)
        s_ref[h] = s
        o = _dot(jnp.broadcast_to(q, (SUBLANES, HEAD_DIM)), s, _HI)[0:1]
        outs.append(_gated_rmsnorm(o, z[:, sl(0)], nw))
    o_all = jnp.concatenate(outs, axis=-1)
    o_ref[...] = jnp.broadcast_to(o_all, (SAMPLE_ROWS, W_DN)).astype(BF16)


def _gdn_sample(proj, logits, state_conv, conv_w, a_log, dt_bias, norm_w, state_dn, layer, *, n_batch):
    smem = pl.BlockSpec(memory_space=pltpu.SMEM)
    return pl.pallas_call(
        _gdn_step_kernel,
        grid=(n_batch,),
        in_specs=[smem, smem,
                  pl.BlockSpec((SAMPLE_ROWS, CONV_CH), lambda b: (b, 0)),
                  pl.BlockSpec((SAMPLE_ROWS, W_DN), lambda b: (b, CONV_CH // W_DN)),
                  pl.BlockSpec((SAMPLE_ROWS, LANES), lambda b: (b, 0)),
                  pl.BlockSpec((None, None, CONV_W - 1, CONV_CH), lambda b: (layer, b, 0, 0)),
                  pl.BlockSpec((CONV_W, CONV_CH), lambda b: (0, 0)),
                  pl.BlockSpec((1, HEAD_DIM), lambda b: (0, 0)),
                  pl.BlockSpec((None, None, H_DN, HEAD_DIM, HEAD_DIM), lambda b: (layer, b, 0, 0, 0))],
        out_specs=[pl.BlockSpec((SAMPLE_ROWS, W_DN), lambda b: (b, 0)),
                   pl.BlockSpec((None, H_DN, HEAD_DIM, HEAD_DIM), lambda b: (b, 0, 0, 0)),
                   pl.BlockSpec((None, CONV_W - 1, CONV_CH), lambda b: (b, 0, 0))],
        out_shape=[jax.ShapeDtypeStruct((n_batch * SAMPLE_ROWS, W_DN), BF16),
                   jax.ShapeDtypeStruct((n_batch, H_DN, HEAD_DIM, HEAD_DIM), F32),
                   jax.ShapeDtypeStruct((n_batch, CONV_W - 1, CONV_CH), F32)],
        compiler_params=_cparams(("arbitrary",)),
        name="gdn_sample",
    )(a_log, dt_bias, proj, proj, logits, state_conv, conv_w, norm_w, state_dn)


PAGES_PER_BLOCK = MOBA_BLOCK // PAGE_SIZE
KMEAN_PAGES = 16
KMEAN_BLOCKS = KMEAN_PAGES // PAGES_PER_BLOCK


def _kmean_kernel(pt_ref, *refs):
    del pt_ref
    pages, o_ref = refs[:KMEAN_PAGES], refs[KMEAN_PAGES]
    for j in range(KMEAN_BLOCKS):
        tot = jnp.sum(pages[PAGES_PER_BLOCK * j][...], axis=0)
        for p in range(1, PAGES_PER_BLOCK):
            tot = tot + jnp.sum(pages[PAGES_PER_BLOCK * j + p][...], axis=0)
        o_ref[j] = tot * (1.0 / MOBA_BLOCK)


def _kmean_sample(page_table, cache_k, layer, *, n_batch, n_pages):
    steps = n_pages // KMEAN_PAGES
    page = lambda p: pl.BlockSpec((None, None, PAGE_SIZE, H_MOBA, HEAD_DIM),
                                  lambda b, s, pt: (layer, pt[b, s * KMEAN_PAGES + p], 0, 0, 0))
    return pl.pallas_call(
        _kmean_kernel,
        grid_spec=pltpu.PrefetchScalarGridSpec(
            num_scalar_prefetch=1,
            grid=(n_batch, steps),
            in_specs=[page(p) for p in range(KMEAN_PAGES)],
            out_specs=pl.BlockSpec((None, KMEAN_BLOCKS, H_MOBA, HEAD_DIM), lambda b, s, pt: (b, s, 0, 0)),
        ),
        out_shape=jax.ShapeDtypeStruct((n_batch, n_pages // PAGES_PER_BLOCK, H_MOBA, HEAD_DIM), F32),
        compiler_params=_cparams(("arbitrary", "arbitrary")),
        name="kmean_sample",
    )(page_table, *([cache_k] * KMEAN_PAGES))


def _topk_kernel(q_ref, kmean_ref, o_ref, *, n_past):
    blk_id = lax.broadcasted_iota(jnp.int32, (n_past, 1), 0)
    sub = lax.broadcasted_iota(jnp.int32, (SUBLANES, LANES), 0)
    lane = lax.broadcasted_iota(jnp.int32, (SUBLANES, LANES), 1)
    out = jnp.zeros((SUBLANES, LANES), jnp.int32)
    for h in range(H_MOBA):
        gate = jnp.sum(kmean_ref[:, h, :] * q_ref[h:h + 1, :], axis=-1, keepdims=True)
        for t in range(MOBA_TOPK):
            best = jnp.max(gate, axis=0, keepdims=True)
            idx = jnp.min(jnp.where(gate == best, blk_id, n_past), axis=0, keepdims=True)
            out = jnp.where((sub == t) & (lane == h), idx, out)
            gate = jnp.where(blk_id == idx, -jnp.inf, gate)
    o_ref[...] = out


def _topk_sample(q_heads, kmean, *, n_batch):
    n_past = kmean.shape[1]
    return pl.pallas_call(
        functools.partial(_topk_kernel, n_past=n_past),
        grid=(n_batch,),
        in_specs=[pl.BlockSpec((None, H_MOBA, HEAD_DIM), lambda b: (b, 0, 0)),
                  pl.BlockSpec((None, n_past, H_MOBA, HEAD_DIM), lambda b: (b, 0, 0, 0))],
        out_specs=pl.BlockSpec((None, SUBLANES, LANES), lambda b: (b, 0, 0)),
        out_shape=jax.ShapeDtypeStruct((n_batch, SUBLANES, LANES), jnp.int32),
        compiler_params=_cparams(("arbitrary",)),
        name="topk_sample",
    )(q_heads, kmean)


N_SEL_PAGES = MOBA_TOPK * PAGES_PER_BLOCK


def _attend_kernel(pt_ref, top_ref, far_ref, qkv_ref, bias_ref, ck_ref, cv_ref, o_ref, kbuf, vbuf, sem,
                   *, layer, n_past):
    b = pl.program_id(0)
    scale = HEAD_DIM ** -0.5

    def page_copies(h, t, p):
        slot = h * N_SEL_PAGES + t * PAGES_PER_BLOCK + p
        pg = pt_ref[b, top_ref[b, t * LANES + h] * PAGES_PER_BLOCK + p]
        return (pltpu.make_async_copy(ck_ref.at[layer, pg, :, h, :], kbuf.at[slot], sem.at[0, slot]),
                pltpu.make_async_copy(cv_ref.at[layer, pg, :, h, :], vbuf.at[slot], sem.at[1, slot]))

    slots = [(h, t, p) for h in range(H_MOBA) for t in range(MOBA_TOPK) for p in range(PAGES_PER_BLOCK)]
    for key in slots:
        for cp in page_copies(*key):
            cp.start()

    for h in range(H_MOBA):
        for t in range(MOBA_TOPK):
            for p in range(PAGES_PER_BLOCK):
                for cp in page_copies(h, t, p):
                    cp.wait()
        q = qkv_ref[h, 0:1, :]
        k_new = qkv_ref[h, 1:2, :]
        v_new = qkv_ref[h, 2:3, :]
        q16 = jnp.broadcast_to(q, (SUBLANES, HEAD_DIM)).astype(BF16)
        s_own = jnp.sum(q.astype(BF16).astype(F32) * k_new.astype(BF16).astype(F32), axis=-1, keepdims=True) * scale
        s_own = s_own + bias_ref[h, 0, 0:1, 0:1]
        near = bias_ref[h, 1, 0:1, :]
        scores = []
        for t in range(MOBA_TOPK):
            bias = jnp.where(top_ref[b, t * LANES + h] == n_past - 1, near, far_ref[h])
            for p in range(PAGES_PER_BLOCK):
                kp = kbuf[h * N_SEL_PAGES + t * PAGES_PER_BLOCK + p].astype(BF16)
                sp = _dot_nt(q16, kp)[0:1] * scale
                scores.append(sp + bias[:, p * PAGE_SIZE:(p + 1) * PAGE_SIZE])
        m = s_own
        for sp in scores:
            m = jnp.maximum(m, jnp.max(sp, axis=-1, keepdims=True))
        p_own = jnp.exp(s_own - m)
        denom = p_own
        acc = p_own.astype(BF16).astype(F32) * v_new.astype(BF16).astype(F32)
        for i, sp in enumerate(scores):
            pp = jnp.exp(sp - m)
            denom = denom + jnp.sum(pp, axis=-1, keepdims=True)
            pp8 = jnp.broadcast_to(pp, (SUBLANES, PAGE_SIZE)).astype(BF16)
            acc = acc + _dot(pp8, vbuf[h * N_SEL_PAGES + i].astype(BF16))[0:1]
        o_ref[h] = jnp.broadcast_to(acc / denom, (SUBLANES, HEAD_DIM))


def _attend_sample(page_table, top_idx, far_bias, qkv_new, bias_tiles, cache_k, cache_v, layer, *, n_batch, n_past):
    n_slots = H_MOBA * N_SEL_PAGES
    return pl.pallas_call(
        functools.partial(_attend_kernel, layer=layer, n_past=n_past),
        grid_spec=pltpu.PrefetchScalarGridSpec(
            num_scalar_prefetch=2,
            grid=(n_batch,),
            in_specs=[pl.BlockSpec(memory_space=pltpu.SMEM),
                      pl.BlockSpec((None, H_MOBA, SUBLANES, HEAD_DIM), lambda b, pt, top: (b, 0, 0, 0)),
                      pl.BlockSpec((H_MOBA, 2, SUBLANES, MOBA_BLOCK), lambda b, pt, top: (0, 0, 0, 0)),
                      pl.BlockSpec(memory_space=pl.ANY), pl.BlockSpec(memory_space=pl.ANY)],
            out_specs=pl.BlockSpec((None, H_MOBA, SUBLANES, HEAD_DIM), lambda b, pt, top: (b, 0, 0, 0)),
            scratch_shapes=[pltpu.VMEM((n_slots, PAGE_SIZE, HEAD_DIM), F32),
                            pltpu.VMEM((n_slots, PAGE_SIZE, HEAD_DIM), F32),
                            pltpu.SemaphoreType.DMA((2, n_slots))],
        ),
        out_shape=jax.ShapeDtypeStruct((n_batch, H_MOBA, SUBLANES, HEAD_DIM), F32),
        compiler_params=_cparams(("arbitrary",)),
        name="attend_sample",
    )(page_table, top_idx, far_bias, qkv_new, bias_tiles, cache_k, cache_v)


def _pick_tile(m, pref):
    return pref if m % pref == 0 else m


def _rows_after_mixer(x, xb, mix_a, mix_b, lw, *, n_batch, mem_k, mem_v):
    m = x.shape[0]
    x, xb = _mm2_ln(mix_a, mix_b, lw["wo_a"], lw["wo_b"], x, lw["g0"], lw["b0"], tm=_pick_tile(m, 256))
    x, xb = _cross_attention(xb, x, lw["wq"], mem_k, mem_v, lw["wo_x"], lw["g1"], lw["b1"],
                             n_batch=n_batch, tq=min(256, m // n_batch))
    x, xb = _ffn(xb, x, lw["w1"], lw["w2"], lw["g2"], lw["b2"], tm=_pick_tile(m, 512), fc=512)
    return x, xb


def kernel(x_prompt, x_sample, cache_k, cache_v, cache_mem_k, cache_mem_v, state_dn, state_conv, page_table,
           mem_prompt, w_in, dn_conv_w, dn_a_log, dn_dt_bias, dn_norm_w, w_out, rel_bias, x_wq, x_wkv, x_wo,
           ffn_w1, ffn_w2, ln_g, ln_b):
    n_batch, seq, _ = x_prompt.shape
    n_dec = x_sample.shape[0]
    n_pages = page_table.shape[1]
    n_past = n_pages // PAGES_PER_BLOCK
    mp = n_batch * seq

    bias_tiles = _bias_tiles(rel_bias)
    far_bias = rel_bias[N_BUCKETS - 1]
    mem_k2 = cache_mem_k.reshape(DEPTH, n_dec, N_MEM, W_X)
    mem_v2 = cache_mem_v.reshape(DEPTH, n_dec, N_MEM, W_X)
    mem16 = mem_prompt.reshape(n_batch * N_MEM, D_MODEL).astype(BF16)

    xp = x_prompt.reshape(mp, D_MODEL)
    xpb = xp.astype(BF16)
    xs = jnp.pad(x_sample, ((0, 0), (0, SAMPLE_ROWS - x_sample.shape[1]), (0, 0))).reshape(n_dec * SAMPLE_ROWS, D_MODEL)
    xsb = xs.astype(BF16)

    outs = {k: [] for k in ("kp", "vp", "ks", "vs", "mkp", "mvp", "sp", "ss", "cp", "cs")}
    for l in range(DEPTH):
        w_main = jnp.concatenate([w_in[l][:, :OFF_BA], w_in[l][:, OFF_QB:]], axis=1).astype(BF16)
        w_logit = jnp.pad(w_in[l][:, OFF_BA:OFF_QB], ((0, 0), (0, LANES - 2 * H_DN))).astype(BF16)
        lw = dict(
            wo_a=w_out[l][:W_DN].astype(BF16), wo_b=w_out[l][W_DN:].astype(BF16),
            wq=x_wq[l].astype(BF16), wo_x=x_wo[l].astype(BF16),
            w1=ffn_w1[l].astype(BF16), w2=ffn_w2[l].astype(BF16),
            g0=ln_g[l, 0][None], b0=ln_b[l, 0][None], g1=ln_g[l, 1][None], b1=ln_b[l, 1][None],
            g2=ln_g[l, 2][None], b2=ln_b[l, 2][None])
        norm_w = dn_norm_w[l][None]

        proj = _matmul(xpb, w_main, tm=512, tn=W_MAIN // 4, out_dtype=F32)
        logits = _matmul(xpb, w_logit, tm=512, tn=LANES, out_dtype=F32)
        oa, s_fin = _gdn_prompt(proj, logits, dn_conv_w[l], dn_a_log[l], dn_dt_bias[l], norm_w,
                                n_batch=n_batch, seq=seq, rows=256)
        ob = _moba_prompt(proj, bias_tiles, far_bias, n_batch=n_batch, seq=seq)
        proj3 = proj.reshape(n_batch, seq, W_MAIN)
        outs["kp"].append(proj3[:, :, 4 * W_DN + W_MOBA:4 * W_DN + 2 * W_MOBA].reshape(n_batch, seq, H_MOBA, HEAD_DIM))
        outs["vp"].append(proj3[:, :, 4 * W_DN + 2 * W_MOBA:].reshape(n_batch, seq, H_MOBA, HEAD_DIM))
        outs["sp"].append(s_fin)
        outs["cp"].append(proj3[:, seq - (CONV_W - 1):, :CONV_CH])

        proj_s = _matmul(xsb, w_main, tm=n_dec * SAMPLE_ROWS, tn=W_MAIN // 4, out_dtype=F32)
        logits_s = _matmul(xsb, w_logit, tm=n_dec * SAMPLE_ROWS, tn=LANES, out_dtype=F32)
        oa_s, s_new, conv_new = _gdn_sample(proj_s, logits_s, state_conv, dn_conv_w[l], dn_a_log[l], dn_dt_bias[l],
                                            norm_w, state_dn, l, n_batch=n_dec)
        kmean = _kmean_sample(page_table, cache_k, l, n_batch=n_dec, n_pages=n_pages)
        tok = proj_s.reshape(n_dec, SAMPLE_ROWS, W_MAIN)[:, 0, 4 * W_DN:]
        top_idx = _topk_sample(tok[:, :W_MOBA].reshape(n_dec, H_MOBA, HEAD_DIM), kmean, n_batch=n_dec)
        qkv_new = tok.reshape(n_dec, 3, H_MOBA, HEAD_DIM).transpose(0, 2, 1, 3)
        qkv_new = jnp.pad(qkv_new, ((0, 0), (0, 0), (0, SUBLANES - 3), (0, 0)))
        ob_s = _attend_sample(page_table, top_idx.reshape(n_dec, SUBLANES * LANES), far_bias, qkv_new, bias_tiles,
                              cache_k, cache_v, l, n_batch=n_dec, n_past=n_past)
        ob_s = jnp.broadcast_to(ob_s[:, :, 0, :].reshape(n_dec, 1, W_MOBA), (n_dec, SAMPLE_ROWS, W_MOBA))
        ob_s = ob_s.reshape(n_dec * SAMPLE_ROWS, W_MOBA).astype(BF16)
        outs["ks"].append(tok[:, W_MOBA:2 * W_MOBA].reshape(n_dec, 1, H_MOBA, HEAD_DIM))
        outs["vs"].append(tok[:, 2 * W_MOBA:].reshape(n_dec, 1, H_MOBA, HEAD_DIM))
        outs["ss"].append(s_new)
        outs["cs"].append(conv_new)

        kv = _matmul(mem16, x_wkv[l].astype(BF16), tm=_pick_tile(mem16.shape[0], 512), tn=2 * W_X, out_dtype=F32)
        kv3 = kv.reshape(n_batch, N_MEM, 2 * W_X)
        mkp, mvp = kv3[:, :, :W_X], kv3[:, :, W_X:]
        outs["mkp"].append(mkp.reshape(n_batch, N_MEM, H_X, HEAD_DIM))
        outs["mvp"].append(mvp.reshape(n_batch, N_MEM, H_X, HEAD_DIM))

        xp, xpb = _rows_after_mixer(xp, xpb, oa, ob, lw, n_batch=n_batch, mem_k=(mkp, ()), mem_v=(mvp, ()))
        xs, xsb = _rows_after_mixer(xs, xsb, oa_s, ob_s, lw, n_batch=n_dec, mem_k=(mem_k2, (l,)), mem_v=(mem_v2, (l,)))

    y_prompt = xp.reshape(n_batch, seq, D_MODEL)
    y_sample = xs.reshape(n_dec, SAMPLE_ROWS, D_MODEL)[:, :x_sample.shape[1]]
    st = lambda k: jnp.stack(outs[k])
    return (y_prompt, y_sample, st("kp"), st("vp"), st("ks"), st("vs"), st("mkp"), st("mvp"),
            st("sp"), st("ss"), st("cp"), st("cs"))
```
